```python
import math
import jax, jax.numpy as jnp
from jax import lax
import numpy as np

D_MODEL = 2048
BATCH = 4
SEQ = 4096
DEPTH = 4

CTX_LEN = 256
GRID_W = 64
Q_BLOCK = 128
ROPE_THETA = 10000.0
NORM_EPS = 1e-5

MLA_HEADS = 8
MLA_Q_LORA = 512
MLA_KV_LORA = 512
MLA_NOPE = 128
MLA_ROPE = 64
MLA_V = 128
MLA_QK = MLA_NOPE + MLA_ROPE

SGU_CHUNK = 128
SGU_GROUPS = 8
SGU_GROUP_W = 128
SGU_W = SGU_GROUPS * SGU_GROUP_W

DIFF_HEADS = 4
DIFF_HEAD_DIM = 128
DIFF_V = 2 * DIFF_HEAD_DIM

N_BRANCH = 3
BRANCH_W = 1024
FFN_HIDDEN = -(-8 * D_MODEL // (3 * 256)) * 256

IN_SIZES = [MLA_Q_LORA, MLA_KV_LORA, MLA_ROPE, 2 * SGU_W,
            DIFF_HEADS * 2 * DIFF_HEAD_DIM, DIFF_HEADS * 2 * DIFF_HEAD_DIM,
            DIFF_HEADS * DIFF_V, N_BRANCH * D_MODEL]
IN_W = int(sum(IN_SIZES))
IN_SPLITS = [int(s) for s in np.cumsum(IN_SIZES)[:-1]]

DEEPNORM_ALPHA = (2 * DEPTH) ** 0.25
DEEPNORM_BETA = (8 * DEPTH) ** -0.25

kernel_name = 'hybrid_mla_sgu_diffattn_deepnorm'


def layer_norm(x, g, b):
    xf = x.astype(jnp.float32)
    mu = jnp.mean(xf, axis=-1, keepdims=True)
    var = jnp.mean(jnp.square(xf - mu), axis=-1, keepdims=True)
    return ((xf - mu) * lax.rsqrt(var + NORM_EPS)).astype(x.dtype) * g + b


def rms_norm(x, g):
    xf = x.astype(jnp.float32)
    return (xf * lax.rsqrt(jnp.mean(xf * xf, axis=-1, keepdims=True) + NORM_EPS)).astype(x.dtype) * g


def axial_rope_tables(rows, dim, dtype):
    r = jnp.repeat(jnp.arange(rows, dtype=jnp.float32), GRID_W)
    col = jnp.tile(jnp.arange(GRID_W, dtype=jnp.float32), rows)
    quarter = dim // 4
    inv = ROPE_THETA ** (-jnp.arange(quarter, dtype=jnp.float32) / quarter)
    ar = r[:, None] * inv
    ac = col[:, None] * inv
    ang = jnp.concatenate([ar, ar, ac, ac], axis=-1)
    return jnp.cos(ang).astype(dtype), jnp.sin(ang).astype(dtype)


def apply_rope(x, cos, sin):
    xa = x.reshape(x.shape[:-1] + (2, 2, x.shape[-1] // 4))
    rot = jnp.stack([-xa[..., 1, :], xa[..., 0, :]], axis=-2).reshape(x.shape)
    return x * cos + rot * sin


def block_attention(q, k, v):
    B, H, Sq, dk = q.shape
    nb = Sq // Q_BLOCK
    scale = dk ** -0.5
    qb = q.reshape(B, H, nb, Q_BLOCK, dk).transpose(2, 0, 1, 3, 4)

    def one_block(qblk):
        s = jnp.einsum('bhqd,bhkd->bhqk', qblk, k, preferred_element_type=jnp.float32) * scale
        p = jax.nn.softmax(s, axis=-1).astype(v.dtype)
        return jnp.einsum('bhqk,bhkd->bhqd', p, v)

    o = lax.map(one_block, qb)
    return o.transpose(1, 2, 0, 3, 4).reshape(B, H, Sq, v.shape[-1])


def heads_to_tokens(o):
    B, H, S, d = o.shape
    return o.transpose(0, 2, 1, 3).reshape(B, S, H * d)


def mla_qkv(zq, zkv, zkr, q_norm, w_uq, kv_norm, w_ukv, rope):
    B, S, _ = zq.shape
    q = (rms_norm(zq, q_norm) @ w_uq).reshape(B, S, MLA_HEADS, MLA_QK).transpose(0, 2, 1, 3)
    kv = (rms_norm(zkv, kv_norm) @ w_ukv).reshape(B, S, MLA_HEADS, MLA_NOPE + MLA_V).transpose(0, 2, 1, 3)
    q_nope, q_rope = q[..., :MLA_NOPE], q[..., MLA_NOPE:]
    k_nope, v = kv[..., :MLA_NOPE], kv[..., MLA_NOPE:]
    k_rope = zkr[:, None]
    if rope is not None:
        cos, sin = rope
        q_rope = apply_rope(q_rope, cos, sin)
        k_rope = apply_rope(k_rope, cos, sin)
    q = jnp.concatenate([q_nope, q_rope], axis=-1)
    k = jnp.concatenate([k_nope, jnp.broadcast_to(k_rope, (B, MLA_HEADS, S, MLA_ROPE))], axis=-1)
    return q, k, v


def diff_qkv(zq, zk, zv, rope):
    B, S, _ = zq.shape
    q = zq.reshape(B, S, DIFF_HEADS, 2, DIFF_HEAD_DIM).transpose(3, 0, 2, 1, 4)
    k = zk.reshape(B, S, DIFF_HEADS, 2, DIFF_HEAD_DIM).transpose(3, 0, 2, 1, 4)
    if rope is not None:
        cos, sin = rope
        q = apply_rope(q, cos, sin)
        k = apply_rope(k, cos, sin)
    v = zv.reshape(B, S, DIFF_HEADS, DIFF_V).transpose(0, 2, 1, 3)
    return q, k, v


def diff_lambda(lp, lam_init):
    lpf = lp.astype(jnp.float32)
    return jnp.exp(jnp.sum(lpf[0] * lpf[1])) - jnp.exp(jnp.sum(lpf[2] * lpf[3])) + lam_init


def diff_combine(o1, o2, lam, subln_g, lam_init):
    o = rms_norm(o1 - lam * o2, subln_g) * (1.0 - lam_init)
    return heads_to_tokens(o)


def spatial_gating(z, ln_g, ln_b, w_s, b_s):
    B, S, _ = z.shape
    u, v = jnp.split(jax.nn.gelu(z), 2, axis=-1)
    v = layer_norm(v, ln_g, ln_b)
    vc = v.reshape(B, S // SGU_CHUNK, SGU_CHUNK, SGU_GROUPS, SGU_GROUP_W)
    s = jnp.einsum('gpq,bnqgc->bnpgc', w_s, vc) + b_s.T[:, :, None]
    return u * s.reshape(B, S, SGU_W)


def merge_branches(z_gate, ya, ys, yd, w_branch, w_out):
    gates = jax.nn.sigmoid(z_gate).reshape(z_gate.shape[:-1] + (N_BRANCH, D_MODEL))
    y = jnp.stack([ya, ys, yd], axis=-2)
    proj = jnp.einsum('bsnc,ncd->bsnd', y, w_branch)
    return jnp.sum(gates * proj, axis=-2) @ w_out


def swiglu(h, w_gu, w_down):
    gt, up = jnp.split(h @ w_gu, 2, axis=-1)
    return (jax.nn.silu(gt) * up) @ w_down


def setup_inputs(seed: int = 0) -> dict:
    key = jax.random.key(seed)
    ks = iter(jax.random.split(key, 32))
    L, D = DEPTH, D_MODEL
    beta = DEEPNORM_BETA

    def nrm(shape, s):
        return jax.random.normal(next(ks), shape, jnp.float32) * s

    return {
        'x': nrm((BATCH, SEQ, D), 1.0),
        'c': nrm((BATCH, D), 1.0),
        'ctx': nrm((BATCH, CTX_LEN, D), 1.0),
        'c_ctx': nrm((D,), 1.0),
        'ada_w': nrm((L, D, 6 * D), 0.5 * D ** -0.5),
        'ada_b': nrm((L, 6 * D), 0.02),
        'w_in': nrm((L, D, IN_W), D ** -0.5),
        'mla_q_norm': 1.0 + nrm((L, MLA_Q_LORA), 0.1),
        'mla_w_uq': nrm((L, MLA_Q_LORA, MLA_HEADS * MLA_QK), MLA_Q_LORA ** -0.5),
        'mla_kv_norm': 1.0 + nrm((L, MLA_KV_LORA), 0.1),
        'mla_w_ukv': nrm((L, MLA_KV_LORA, MLA_HEADS * (MLA_NOPE + MLA_V)), MLA_KV_LORA ** -0.5),
        'sgu_ln_g': 1.0 + nrm((L, SGU_W), 0.1),
        'sgu_ln_b': nrm((L, SGU_W), 0.02),
        'sgu_w': nrm((L, SGU_GROUPS, SGU_CHUNK, SGU_CHUNK), 0.5 * SGU_CHUNK ** -0.5),
        'sgu_b': 1.0 + nrm((L, SGU_GROUPS, SGU_CHUNK), 0.1),
        'diff_lam': nrm((L, 4, DIFF_HEAD_DIM), 0.1),
        'diff_subln': 1.0 + nrm((L, DIFF_V), 0.1),
        'w_branch': nrm((L, N_BRANCH, BRANCH_W, D), beta * BRANCH_W ** -0.5),
        'w_out': nrm((L, D, D), beta * D ** -0.5),
        'ln1_g': 1.0 + nrm((L, D), 0.1),
        'ln1_b': nrm((L, D), 0.02),
        'ffn_w_gu': nrm((L, D, 2 * FFN_HIDDEN), D ** -0.5),
        'ffn_w_down': nrm((L, FFN_HIDDEN, D), beta * FFN_HIDDEN ** -0.5),
        'ln2_g': 1.0 + nrm((L, D), 0.1),
        'ln2_b': nrm((L, D), 0.02),
    }


def reference(x, c, ctx, c_ctx, ada_w, ada_b, w_in, mla_q_norm, mla_w_uq, mla_kv_norm, mla_w_ukv,
              sgu_ln_g, sgu_ln_b, sgu_w, sgu_b, diff_lam, diff_subln, w_branch, w_out,
              ln1_g, ln1_b, ffn_w_gu, ffn_w_down, ln2_g, ln2_b):
    n_lat = x.shape[1]
    rows = n_lat // GRID_W
    rope_mla = axial_rope_tables(rows, MLA_ROPE, x.dtype)
    rope_diff = axial_rope_tables(rows, DIFF_HEAD_DIM, x.dtype)
    silu_c = jax.nn.silu(c)
    silu_cc = jax.nn.silu(c_ctx)
    h_lat, h_ctx = x, ctx
    for l in range(DEPTH):
        ctx_out = l < DEPTH - 1
        lam_init = 0.8 - 0.6 * math.exp(-0.3 * l)
        lam = diff_lambda(diff_lam[l], lam_init).astype(x.dtype)
        sh1, sc1, g1, sh2, sc2, g2 = jnp.split((silu_c @ ada_w[l] + ada_b[l])[:, None, :], 6, axis=-1)
        csh1, csc1, cg1, csh2, csc2, cg2 = jnp.split(silu_cc @ ada_w[l] + ada_b[l], 6, axis=-1)

        z_lat = jnp.split((h_lat * (1 + sc1) + sh1) @ w_in[l], IN_SPLITS, axis=-1)
        z_ctx = jnp.split((h_ctx * (1 + csc1) + csh1) @ w_in[l], IN_SPLITS, axis=-1)
        mla_w = (mla_q_norm[l], mla_w_uq[l], mla_kv_norm[l], mla_w_ukv[l])
        qa_l, ka_l, va_l = mla_qkv(z_lat[0], z_lat[1], z_lat[2], *mla_w, rope_mla)
        qa_c, ka_c, va_c = mla_qkv(z_ctx[0], z_ctx[1], z_ctx[2], *mla_w, None)
        qd_l, kd_l, vd_l = diff_qkv(z_lat[4], z_lat[5], z_lat[6], rope_diff)
        qd_c, kd_c, vd_c = diff_qkv(z_ctx[4], z_ctx[5], z_ctx[6], None)

        ya_l = heads_to_tokens(block_attention(qa_l, jnp.concatenate([ka_c, ka_l], axis=2),
                                               jnp.concatenate([va_c, va_l], axis=2)))
        kd_all = jnp.concatenate([kd_c, kd_l], axis=3)
        vd_all = jnp.concatenate([vd_c, vd_l], axis=2)
        yd_l = diff_combine(block_attention(qd_l[0], kd_all[0], vd_all),
                            block_attention(qd_l[1], kd_all[1], vd_all), lam, diff_subln[l], lam_init)
        ys_l = spatial_gating(z_lat[3], sgu_ln_g[l], sgu_ln_b[l], sgu_w[l], sgu_b[l])
        mix_lat = merge_branches(z_lat[7], ya_l, ys_l, yd_l, w_branch[l], w_out[l])

        if ctx_out:
            ya_c = heads_to_tokens(block_attention(qa_c, ka_c, va_c))
            yd_c = diff_combine(block_attention(qd_c[0], kd_c[0], vd_c),
                                block_attention(qd_c[1], kd_c[1], vd_c), lam, diff_subln[l], lam_init)
            ys_c = spatial_gating(z_ctx[3], sgu_ln_g[l], sgu_ln_b[l], sgu_w[l], sgu_b[l])
            mix_ctx = merge_branches(z_ctx[7], ya_c, ys_c, yd_c, w_branch[l], w_out[l])
            h_ctx = layer_norm(DEEPNORM_ALPHA * h_ctx + cg1 * mix_ctx, ln1_g[l], ln1_b[l])
            ff_ctx = swiglu(h_ctx * (1 + csc2) + csh2, ffn_w_gu[l], ffn_w_down[l])
            h_ctx = layer_norm(DEEPNORM_ALPHA * h_ctx + cg2 * ff_ctx, ln2_g[l], ln2_b[l])

        h_lat = layer_norm(DEEPNORM_ALPHA * h_lat + g1 * mix_lat, ln1_g[l], ln1_b[l])
        ff_lat = swiglu(h_lat * (1 + sc2) + sh2, ffn_w_gu[l], ffn_w_down[l])
        h_lat = layer_norm(DEEPNORM_ALPHA * h_lat + g2 * ff_lat, ln2_g[l], ln2_b[l])
    return h_lat
```

```python
import functools
import math

import jax
import jax.numpy as jnp
from jax import lax
from jax.experimental import pallas as pl
from jax.experimental.pallas import tpu as pltpu

F32 = jnp.float32
BF16 = jnp.bfloat16

D_MODEL = 2048
BATCH = 4
SEQ = 4096
DEPTH = 4
CTX_LEN = 256
GRID_W = 64
ROPE_THETA = 10000.0
NORM_EPS = 1e-5

MLA_HEADS = 8
MLA_Q_LORA = 512
MLA_KV_LORA = 512
MLA_NOPE = 128
MLA_ROPE = 64
MLA_V = 128
MLA_QK = MLA_NOPE + MLA_ROPE
MLA_QK_PAD = 256

SGU_CHUNK = 128
SGU_GROUPS = 8
SGU_GROUP_W = 128
SGU_W = SGU_GROUPS * SGU_GROUP_W

DIFF_HEADS = 4
DIFF_HEAD_DIM = 128
DIFF_V = 2 * DIFF_HEAD_DIM

N_BRANCH = 3
BRANCH_W = 1024
FFN_HIDDEN = -(-8 * D_MODEL // (3 * 256)) * 256

DEEPNORM_ALPHA = (2 * DEPTH) ** 0.25

N_LAT = BATCH * SEQ
N_CTX = BATCH * CTX_LEN
N_TOK = N_LAT + N_CTX
MOD_ROWS = 8
CTX_MOD_ROW = BATCH

Z_ZQ = 0
Z_ZKV = MLA_Q_LORA
Z_DV = 1024
Z_GATE = 2048
Z_SGU = Z_GATE + N_BRANCH * D_MODEL
Z_DQ = Z_SGU + 2 * SGU_W
Z_DK = Z_DQ + DIFF_HEADS * 2 * DIFF_HEAD_DIM
Z_W = Z_DK + DIFF_HEADS * 2 * DIFF_HEAD_DIM

LOG2E = math.log2(math.e)
LANES = 128
VMEM_LIMIT = 56 * 1024 * 1024


def _cparams(sem):
    return pltpu.CompilerParams(dimension_semantics=sem, vmem_limit_bytes=VMEM_LIMIT)


def _mod_row(i, tm):
    return jnp.where(i < N_LAT // tm, i // (SEQ // tm), CTX_MOD_ROW)


def _mod_spec(layer, col, tm):
    return pl.BlockSpec((None, None, 1, D_MODEL), lambda i, *_: (layer, _mod_row(i, tm), 0, col))


def _vec_spec(width=D_MODEL):
    return pl.BlockSpec((1, width), lambda *_: (0, 0))


def _layer_norm(x, g, b):
    mu = jnp.mean(x, axis=-1, keepdims=True)
    xc = x - mu
    var = jnp.mean(xc * xc, axis=-1, keepdims=True)
    return xc * lax.rsqrt(var + NORM_EPS) * g + b


def _rms_norm(x, g):
    return x * lax.rsqrt(jnp.mean(x * x, axis=-1, keepdims=True) + NORM_EPS) * g


def _rope(x, cos, sin_signed, quarter):
    lane = lax.broadcasted_iota(jnp.int32, x.shape, 1)
    first = (lane & quarter) == 0
    rot = jnp.where(first, pltpu.roll(x, LANES - quarter, 1), pltpu.roll(x, quarter, 1))
    return x * cos + rot * sin_signed


def _mod_kernel(c_ref, w_ref, b_ref, o_ref):
    c = c_ref[...]
    sc = (c * jax.nn.sigmoid(c)).astype(BF16)
    o_ref[...] = jnp.dot(sc, w_ref[...].astype(BF16), preferred_element_type=F32) + b_ref[...]


def _mod_table(c_all, ada_w, ada_b):
    tn = 1024
    return pl.pallas_call(
        _mod_kernel,
        out_shape=jax.ShapeDtypeStruct((DEPTH, MOD_ROWS, 6 * D_MODEL), F32),
        grid=(DEPTH, 6 * D_MODEL // tn),
        in_specs=[
            pl.BlockSpec((MOD_ROWS, D_MODEL), lambda l, j: (0, 0)),
            pl.BlockSpec((None, D_MODEL, tn), lambda l, j: (l, 0, j)),
            pl.BlockSpec((None, 1, tn), lambda l, j: (l, 0, j)),
        ],
        out_specs=pl.BlockSpec((None, MOD_ROWS, tn), lambda l, j: (l, 0, j)),
        compiler_params=_cparams(("parallel", "parallel")),
        name="mod_table",
    )(c_all, ada_w, ada_b.reshape(DEPTH, 1, 6 * D_MODEL))


def _modulate_kernel(h_ref, sh_ref, sc_ref, o_ref):
    o_ref[...] = (h_ref[...] * (1.0 + sc_ref[...]) + sh_ref[...]).astype(BF16)


def _modulate(h, mods, layer):
    tm = 1024
    return pl.pallas_call(
        _modulate_kernel,
        out_shape=jax.ShapeDtypeStruct((N_TOK, D_MODEL), BF16),
        grid=(N_TOK // tm,),
        in_specs=[
            pl.BlockSpec((tm, D_MODEL), lambda i: (i, 0)),
            _mod_spec(layer, 0, tm),
            _mod_spec(layer, 1, tm),
        ],
        out_specs=pl.BlockSpec((tm, D_MODEL), lambda i: (i, 0)),
        compiler_params=_cparams(("parallel",)),
        name="modulate",
    )(h, mods, mods)


def _matmul_kernel(x_ref, w_ref, o_ref):
    o_ref[...] = jnp.dot(x_ref[...], w_ref[...], preferred_element_type=F32).astype(o_ref.dtype)


def _in_proj(xm, w):
    tm, tn = 1024, 1024
    return pl.pallas_call(
        _matmul_kernel,
        out_shape=jax.ShapeDtypeStruct((N_TOK, Z_W), BF16),
        grid=(N_TOK // tm, Z_W // tn),
        in_specs=[
            pl.BlockSpec((tm, D_MODEL), lambda i, j: (i, 0)),
            pl.BlockSpec((D_MODEL, tn), lambda i, j: (0, j)),
        ],
        out_specs=pl.BlockSpec((tm, tn), lambda i, j: (i, j)),
        compiler_params=_cparams(("parallel", "arbitrary")),
        name="in_proj",
    )(xm, w)


def _prep_kernel(zq_ref, zkv_ref, dq_ref, dk_ref, xm_ref, wkr_ref, qn_ref, kvn_ref, wuq_ref, wukv_ref,
                 cm_ref, sm_ref, cd_ref, sd_ref, qm_ref, km_ref, vm_ref, qd_ref, kd_ref):
    cm, sm = cm_ref[...], sm_ref[...]
    cd, sd = cd_ref[...], sd_ref[...]
    mla_scale = MLA_QK ** -0.5 * LOG2E
    diff_scale = DIFF_HEAD_DIM ** -0.5 * LOG2E

    qn = _rms_norm(zq_ref[...].astype(F32), qn_ref[...]).astype(BF16)
    q = jnp.dot(qn, wuq_ref[...], preferred_element_type=F32)
    for h in range(MLA_HEADS):
        base = h * MLA_QK_PAD
        qm_ref[:, base:base + LANES] = (q[:, base:base + LANES] * mla_scale).astype(BF16)
        qr = _rope(q[:, base + LANES:base + 2 * LANES], cm, sm, MLA_ROPE // 4)
        qm_ref[:, base + LANES:base + 2 * LANES] = (qr * mla_scale).astype(BF16)

    kvn = _rms_norm(zkv_ref[...].astype(F32), kvn_ref[...]).astype(BF16)
    kv = jnp.dot(kvn, wukv_ref[...], preferred_element_type=F32)
    zkr = jnp.dot(xm_ref[...], wkr_ref[...], preferred_element_type=F32)
    kr = _rope(zkr, cm, sm, MLA_ROPE // 4).astype(BF16)
    for h in range(MLA_HEADS):
        base = h * MLA_QK_PAD
        km_ref[:, base:base + LANES] = kv[:, h * MLA_NOPE:(h + 1) * MLA_NOPE].astype(BF16)
        km_ref[:, base + LANES:base + 2 * LANES] = kr
    vm_ref[...] = kv[:, MLA_HEADS * MLA_NOPE:].astype(BF16)

    for j in range(2 * DIFF_HEADS):
        sl = slice(j * LANES, (j + 1) * LANES)
        qd_ref[:, sl] = (_rope(dq_ref[:, sl].astype(F32), cd, sd, DIFF_HEAD_DIM // 4) * diff_scale).astype(BF16)
        kd_ref[:, sl] = _rope(dk_ref[:, sl].astype(F32), cd, sd, DIFF_HEAD_DIM // 4).astype(BF16)


PREP_TM = 256


def _prep(z, xm, w_kr, q_norm, kv_norm, w_uq, w_ukv, rope_tabs):
    tm = PREP_TM
    lat_tiles = N_LAT // tm
    per_b = SEQ // tm

    def tab_idx(i):
        return (jnp.where(i < lat_tiles, i % per_b, per_b), 0)

    def zspec(width, col):
        return pl.BlockSpec((tm, width), lambda i: (i, col // width))

    tab_spec = pl.BlockSpec((tm, LANES), tab_idx)
    full = lambda shape: pl.BlockSpec(shape, lambda i: (0, 0))
    row = lambda width: pl.BlockSpec((tm, width), lambda i: (i, 0))
    return pl.pallas_call(
        _prep_kernel,
        out_shape=(
            jax.ShapeDtypeStruct((N_TOK, MLA_HEADS * MLA_QK_PAD), BF16),
            jax.ShapeDtypeStruct((N_TOK, MLA_HEADS * MLA_QK_PAD), BF16),
            jax.ShapeDtypeStruct((N_TOK, MLA_HEADS * MLA_V), BF16),
            jax.ShapeDtypeStruct((N_TOK, DIFF_HEADS * 2 * DIFF_HEAD_DIM), BF16),
            jax.ShapeDtypeStruct((N_TOK, DIFF_HEADS * 2 * DIFF_HEAD_DIM), BF16),
        ),
        grid=(N_TOK // tm,),
        in_specs=[
            zspec(MLA_Q_LORA, Z_ZQ), zspec(MLA_KV_LORA, Z_ZKV), zspec(1024, Z_DQ), zspec(1024, Z_DK),
            row(D_MODEL), full((D_MODEL, LANES)), full((1, MLA_Q_LORA)), full((1, MLA_KV_LORA)),
            full((MLA_Q_LORA, MLA_HEADS * MLA_QK_PAD)), full((MLA_KV_LORA, MLA_HEADS * (MLA_NOPE + MLA_V))),
            tab_spec, tab_spec, tab_spec, tab_spec,
        ],
        out_specs=(row(MLA_HEADS * MLA_QK_PAD), row(MLA_HEADS * MLA_QK_PAD), row(MLA_HEADS * MLA_V),
                   row(1024), row(1024)),
        compiler_params=_cparams(("parallel",)),
        name="attn_prep",
    )(z, z, z, z, xm, w_kr, q_norm, kv_norm, w_uq, w_ukv, *rope_tabs)


ATT_TQ = 256
ATT_TK = 512
ATT_Q_TILES = SEQ // ATT_TQ


def _softmax_step(state, s, v):
    mj = jnp.max(s, axis=-1, keepdims=True)
    if state is None:
        m_new = mj
        p = jnp.exp2(s - m_new)
        l_new = jnp.sum(p, axis=-1, keepdims=True)
        acc_new = jnp.dot(p.astype(BF16), v, preferred_element_type=F32)
    else:
        m, l, acc = state
        m_new = jnp.maximum(m, mj)
        alpha = jnp.exp2(m - m_new)
        p = jnp.exp2(s - m_new)
        l_new = alpha * l + jnp.sum(p, axis=-1, keepdims=True)
        acc_new = alpha * acc + jnp.dot(p.astype(BF16), v, preferred_element_type=F32)
    return m_new, l_new, acc_new


def _qk(q, k):
    return lax.dot_general(q, k, (((1,), (1,)), ((), ())), preferred_element_type=F32)


def _key_segments(with_lat):
    segs = [("c", 0, CTX_LEN)]
    if with_lat:
        segs += [("l", j * ATT_TK, ATT_TK) for j in range(SEQ // ATT_TK)]
    return segs


def _mla_attn_kernel(q_ref, kl_ref, kc_ref, vl_ref, vc_ref, o_ref, *, with_ctx):
    def run(with_lat):
        q = q_ref[...]
        state = None
        for which, start, size in _key_segments(with_lat):
            k_ref, v_ref = (kc_ref, vc_ref) if which == "c" else (kl_ref, vl_ref)
            s = _qk(q, k_ref[start:start + size, :])
            state = _softmax_step(state, s, v_ref[start:start + size, :])
        _, l, acc = state
        o_ref[...] = (acc / l).astype(BF16)

    if with_ctx:
        qi = pl.program_id(2)
        pl.when(qi < ATT_Q_TILES)(lambda: run(True))
        pl.when(qi == ATT_Q_TILES)(lambda: run(False))
    else:
        run(True)


def _q_row_block(b, qi):
    return jnp.where(qi < ATT_Q_TILES, b * ATT_Q_TILES + qi, N_LAT // ATT_TQ + b)


def _mla_attn(qm, km, vm, with_ctx):
    n_q = ATT_Q_TILES + (1 if with_ctx else 0)
    ctx_blk = N_LAT // CTX_LEN
    return pl.pallas_call(
        functools.partial(_mla_attn_kernel, with_ctx=with_ctx),
        out_shape=jax.ShapeDtypeStruct((N_TOK if with_ctx else N_LAT, MLA_HEADS * MLA_V), BF16),
        grid=(BATCH, MLA_HEADS, n_q),
        in_specs=[
            pl.BlockSpec((ATT_TQ, MLA_QK_PAD), lambda b, h, qi: (_q_row_block(b, qi), h)),
            pl.BlockSpec((SEQ, MLA_QK_PAD), lambda b, h, qi: (b, h)),
            pl.BlockSpec((CTX_LEN, MLA_QK_PAD), lambda b, h, qi: (ctx_blk + b, h)),
            pl.BlockSpec((SEQ, MLA_V), lambda b, h, qi: (b, h)),
            pl.BlockSpec((CTX_LEN, MLA_V), lambda b, h, qi: (ctx_blk + b, h)),
        ],
        out_specs=pl.BlockSpec((ATT_TQ, MLA_V), lambda b, h, qi: (_q_row_block(b, qi), h)),
        compiler_params=_cparams(("parallel", "parallel", "arbitrary")),
        name="mla_attn",
    )(qm, km, km, vm, vm)


def _diff_attn_kernel(q_ref, kl_ref, kc_ref, vl_ref, vc_ref, lam_ref, g_ref, o_ref, *, with_ctx, lam_init):
    lp = lam_ref[...]
    lam = (jnp.exp(jnp.sum(lp[0:1] * lp[1:2], axis=-1, keepdims=True))
           - jnp.exp(jnp.sum(lp[2:3] * lp[3:4], axis=-1, keepdims=True)) + lam_init)

    def run(with_lat):
        q = q_ref[...]
        outs = []
        for mp in range(2):
            lanes = slice(mp * DIFF_HEAD_DIM, (mp + 1) * DIFF_HEAD_DIM)
            state = None
            for which, start, size in _key_segments(with_lat):
                k_ref, v_ref = (kc_ref, vc_ref) if which == "c" else (kl_ref, vl_ref)
                s = _qk(q[:, lanes], k_ref[start:start + size, lanes])
                state = _softmax_step(state, s, v_ref[start:start + size, :])
            _, l, acc = state
            outs.append(acc / l)
        o = outs[0] - lam * outs[1]
        o = _rms_norm(o, g_ref[...]) * (1.0 - lam_init)
        o_ref[...] = o.astype(BF16)

    if with_ctx:
        qi = pl.program_id(2)
        pl.when(qi < ATT_Q_TILES)(lambda: run(True))
        pl.when(qi == ATT_Q_TILES)(lambda: run(False))
    else:
        run(True)


def _diff_attn(qd, kd, z, lam_p, subln, with_ctx, lam_init):
    n_q = ATT_Q_TILES + (1 if with_ctx else 0)
    ctx_blk = N_LAT // CTX_LEN
    v_col = Z_DV // DIFF_V
    return pl.pallas_call(
        functools.partial(_diff_attn_kernel, with_ctx=with_ctx, lam_init=lam_init),
        out_shape=jax.ShapeDtypeStruct((N_TOK if with_ctx else N_LAT, DIFF_HEADS * DIFF_V), BF16),
        grid=(BATCH, DIFF_HEADS, n_q),
        in_specs=[
            pl.BlockSpec((ATT_TQ, 2 * DIFF_HEAD_DIM), lambda b, h, qi: (_q_row_block(b, qi), h)),
            pl.BlockSpec((SEQ, 2 * DIFF_HEAD_DIM), lambda b, h, qi: (b, h)),
            pl.BlockSpec((CTX_LEN, 2 * DIFF_HEAD_DIM), lambda b, h, qi: (ctx_blk + b, h)),
            pl.BlockSpec((SEQ, DIFF_V), lambda b, h, qi: (b, v_col + h)),
            pl.BlockSpec((CTX_LEN, DIFF_V), lambda b, h, qi: (ctx_blk + b, v_col + h)),
            pl.BlockSpec((4, DIFF_HEAD_DIM), lambda b, h, qi: (0, 0)),
            pl.BlockSpec((1, DIFF_V), lambda b, h, qi: (0, 0)),
        ],
        out_specs=pl.BlockSpec((ATT_TQ, DIFF_V), lambda b, h, qi: (_q_row_block(b, qi), h)),
        compiler_params=_cparams(("parallel", "parallel", "arbitrary")),
        name="diff_attn",
    )(qd, kd, kd, z, z, lam_p, subln)


def _gelu_tanh(x):
    return x * (0.5 * (1.0 + jnp.tanh(math.sqrt(2.0 / math.pi) * (x + 0.044715 * (x * x * x)))))


def _sgu_kernel(z_ref, g_ref, b_ref, ws_ref, bs_ref, o_ref, *, n_chunks):
    a = _gelu_tanh(z_ref[...].astype(F32))
    u = a[:, :SGU_W]
    v = _layer_norm(a[:, SGU_W:], g_ref[...], b_ref[...]).astype(BF16)
    bs = bs_ref[...]
    for c in range(n_chunks):
        rows = slice(c * SGU_CHUNK, (c + 1) * SGU_CHUNK)
        for g in range(SGU_GROUPS):
            cols = slice(g * SGU_GROUP_W, (g + 1) * SGU_GROUP_W)
            s = jnp.dot(ws_ref[g], v[rows, cols], preferred_element_type=F32) + bs[:, g:g + 1]
            o_ref[rows, cols] = (u[rows, cols] * s).astype(BF16)


def _sgu(z, ln_g, ln_b, w_s, b_s_t, n_rows):
    tm = 512
    return pl.pallas_call(
        functools.partial(_sgu_kernel, n_chunks=tm // SGU_CHUNK),
        out_shape=jax.ShapeDtypeStruct((n_rows, SGU_W), BF16),
        grid=(n_rows // tm,),
        in_specs=[
            pl.BlockSpec((tm, 2 * SGU_W), lambda i: (i, Z_SGU // (2 * SGU_W))),
            _vec_spec(SGU_W), _vec_spec(SGU_W),
            pl.BlockSpec((SGU_GROUPS, SGU_CHUNK, SGU_CHUNK), lambda i: (0, 0, 0)),
            pl.BlockSpec((SGU_CHUNK, SGU_GROUPS), lambda i: (0, 0)),
        ],
        out_specs=pl.BlockSpec((tm, SGU_W), lambda i: (i, 0)),
        compiler_params=_cparams(("parallel",)),
        name="sgu",
    )(z, ln_g, ln_b, w_s, b_s_t)


def _branch_kernel(ya_ref, ys_ref, yd_ref, g0_ref, g1_ref, g2_ref, w_ref, o_ref):
    acc = None
    for n, (y_ref, g_ref) in enumerate(((ya_ref, g0_ref), (ys_ref, g1_ref), (yd_ref, g2_ref))):
        proj = jnp.dot(y_ref[...], w_ref[n], preferred_element_type=F32)
        term = jax.nn.sigmoid(g_ref[...].astype(F32)) * proj
        acc = term if acc is None else acc + term
    o_ref[...] = acc.astype(BF16)


def _branch_merge(ya, ys, yd, z, w_branch, n_rows):
    tm, tn = 1024, 512
    n_j = D_MODEL // tn
    y_spec = pl.BlockSpec((tm, BRANCH_W), lambda i, j: (i, 0))

    def gate_spec(n):
        return pl.BlockSpec((tm, tn), lambda i, j: (i, (Z_GATE + n * D_MODEL) // tn + j))

    return pl.pallas_call(
        _branch_kernel,
        out_shape=jax.ShapeDtypeStruct((n_rows, D_MODEL), BF16),
        grid=(n_rows // tm, n_j),
        in_specs=[y_spec, y_spec, y_spec, gate_spec(0), gate_spec(1), gate_spec(2),
                  pl.BlockSpec((N_BRANCH, BRANCH_W, tn), lambda i, j: (0, 0, j))],
        out_specs=pl.BlockSpec((tm, tn), lambda i, j: (i, j)),
        compiler_params=_cparams(("parallel", "arbitrary")),
        name="branch_merge",
    )(ya, ys, yd, z, z, z, w_branch)


def _out_proj_kernel(m_ref, w_ref, h_ref, gate_ref, sc_ref, sh_ref, lg_ref, lb_ref, h1_ref, hm_ref):
    mix = jnp.dot(m_ref[...], w_ref[...], preferred_element_type=F32)
    h1 = _layer_norm(DEEPNORM_ALPHA * h_ref[...] + gate_ref[...] * mix, lg_ref[...], lb_ref[...])
    h1_ref[...] = h1
    hm_ref[...] = (h1 * (1.0 + sc_ref[...]) + sh_ref[...]).astype(BF16)


def _out_proj(m, w_out, h, mods, layer, ln_g, ln_b, n_rows):
    tm = 512
    row = pl.BlockSpec((tm, D_MODEL), lambda i: (i, 0))
    return pl.pallas_call(
        _out_proj_kernel,
        out_shape=(jax.ShapeDtypeStruct((n_rows, D_MODEL), F32), jax.ShapeDtypeStruct((n_rows, D_MODEL), BF16)),
        grid=(n_rows // tm,),
        in_specs=[row, pl.BlockSpec((D_MODEL, D_MODEL), lambda i: (0, 0), pipeline_mode=pl.Buffered(1)), row,
                  _mod_spec(layer, 2, tm), _mod_spec(layer, 4, tm), _mod_spec(layer, 3, tm),
                  _vec_spec(), _vec_spec()],
        out_specs=(row, row),
        compiler_params=_cparams(("parallel",)),
        name="out_proj_ln1",
    )(m, w_out, h, mods, mods, mods, ln_g, ln_b)


def _ffn_up_kernel(x_ref, wg_ref, wu_ref, o_ref):
    x = x_ref[...]
    gt = jnp.dot(x, wg_ref[...], preferred_element_type=F32)
    up = jnp.dot(x, wu_ref[...], preferred_element_type=F32)
    o_ref[...] = (gt * jax.nn.sigmoid(gt) * up).astype(BF16)


def _ffn_up(hm, w_gu, n_rows):
    tm, tn = 1024, 512
    n_j = FFN_HIDDEN // tn
    return pl.pallas_call(
        _ffn_up_kernel,
        out_shape=jax.ShapeDtypeStruct((n_rows, FFN_HIDDEN), BF16),
        grid=(n_rows // tm, n_j),
        in_specs=[pl.BlockSpec((tm, D_MODEL), lambda i, j: (i, 0)),
                  pl.BlockSpec((D_MODEL, tn), lambda i, j: (0, j)),
                  pl.BlockSpec((D_MODEL, tn), lambda i, j: (0, n_j + j))],
        out_specs=pl.BlockSpec((tm, tn), lambda i, j: (i, j)),
        compiler_params=_cparams(("parallel", "arbitrary")),
        name="ffn_up",
    )(hm, w_gu, w_gu)


def _ffn_down_kernel(a_ref, w_ref, h_ref, gate_ref, lg_ref, lb_ref, *rest, emit_next):
    if emit_next:
        sc_ref, sh_ref, h2_ref, xm_ref, acc_ref = rest
    else:
        h2_ref, acc_ref = rest
    k = pl.program_id(1)
    part = jnp.dot(a_ref[...], w_ref[...], preferred_element_type=F32)

    @pl.when(k == 0)
    def _():
        acc_ref[...] = part

    @pl.when(k > 0)
    def _():
        acc_ref[...] += part

    @pl.when(k == pl.num_programs(1) - 1)
    def _():
        h2 = _layer_norm(DEEPNORM_ALPHA * h_ref[...] + gate_ref[...] * acc_ref[...], lg_ref[...], lb_ref[...])
        h2_ref[...] = h2
        if emit_next:
            xm_ref[...] = (h2 * (1.0 + sc_ref[...]) + sh_ref[...]).astype(BF16)


def _ffn_down(act, w_down, h1, mods, layer, ln_g, ln_b, n_rows, emit_next):
    tm, tk = 512, 512
    row = pl.BlockSpec((tm, D_MODEL), lambda i, k: (i, 0))
    in_specs = [pl.BlockSpec((tm, tk), lambda i, k: (i, k)),
                pl.BlockSpec((tk, D_MODEL), lambda i, k: (k, 0)),
                row, _mod_spec(layer, 5, tm), _vec_spec(), _vec_spec()]
    args = [act, w_down, h1, mods, ln_g, ln_b]
    out_shape = [jax.ShapeDtypeStruct((n_rows, D_MODEL), F32)]
    out_specs = [row]
    if emit_next:
        in_specs += [_mod_spec(layer + 1, 1, tm), _mod_spec(layer + 1, 0, tm)]
        args += [mods, mods]
        out_shape.append(jax.ShapeDtypeStruct((n_rows, D_MODEL), BF16))
        out_specs.append(row)
    return pl.pallas_call(
        functools.partial(_ffn_down_kernel, emit_next=emit_next),
        out_shape=tuple(out_shape),
        grid=(n_rows // tm, FFN_HIDDEN // tk),
        in_specs=in_specs,
        out_specs=tuple(out_specs),
        scratch_shapes=[pltpu.VMEM((tm, D_MODEL), F32)],
        compiler_params=_cparams(("parallel", "arbitrary")),
        name="ffn_down_ln2",
    )(*args)


def _rope_tables(dim, pad_rows):
    rows = SEQ // GRID_W
    r = jnp.repeat(jnp.arange(rows, dtype=F32), GRID_W)
    col = jnp.tile(jnp.arange(GRID_W, dtype=F32), rows)
    quarter = dim // 4
    inv = ROPE_THETA ** (-jnp.arange(quarter, dtype=F32) / quarter)
    ar = r[:, None] * inv
    ac = col[:, None] * inv
    ang = jnp.concatenate([ar, ar, ac, ac], axis=-1)
    sign = jnp.where((jnp.arange(dim) // quarter) % 2 == 0, -1.0, 1.0).astype(F32)
    cos = jnp.cos(ang)
    sin = jnp.sin(ang) * sign
    cos = jnp.pad(cos, ((0, pad_rows), (0, LANES - dim)), constant_values=1.0)
    sin = jnp.pad(sin, ((0, pad_rows), (0, LANES - dim)))
    return cos, sin


def _prepare_weights(w_in, mla_w_uq, mla_w_ukv, sgu_w, sgu_b, w_branch, w_out, ffn_w_gu, ffn_w_down):
    o_zkr = MLA_Q_LORA + MLA_KV_LORA
    o_sgu = o_zkr + MLA_ROPE
    o_dq = o_sgu + 2 * SGU_W
    o_dk = o_dq + 1024
    o_dv = o_dk + 1024
    o_gate = o_dv + 1024
    w_in_p = jnp.concatenate(
        [w_in[..., :o_zkr], w_in[..., o_dv:o_gate], w_in[..., o_gate:], w_in[..., o_sgu:o_dq],
         w_in[..., o_dq:o_dk], w_in[..., o_dk:o_dv]], axis=-1).astype(BF16)
    w_kr = jnp.pad(w_in[..., o_zkr:o_sgu], ((0, 0), (0, 0), (0, LANES - MLA_ROPE))).astype(BF16)
    w_uq_p = jnp.pad(mla_w_uq.reshape(DEPTH, MLA_Q_LORA, MLA_HEADS, MLA_QK),
                     ((0, 0), (0, 0), (0, 0), (0, MLA_QK_PAD - MLA_QK)))
    w_uq_p = w_uq_p.reshape(DEPTH, MLA_Q_LORA, MLA_HEADS * MLA_QK_PAD).astype(BF16)
    w_ukv_p = mla_w_ukv.reshape(DEPTH, MLA_KV_LORA, MLA_HEADS, 2, MLA_NOPE).transpose(0, 1, 3, 2, 4)
    w_ukv_p = w_ukv_p.reshape(DEPTH, MLA_KV_LORA, 2 * MLA_HEADS * MLA_NOPE).astype(BF16)
    return dict(
        w_in=w_in_p, w_kr=w_kr, w_uq=w_uq_p, w_ukv=w_ukv_p,
        sgu_w=sgu_w.astype(BF16), sgu_b_t=jnp.swapaxes(sgu_b, 1, 2),
        w_branch=w_branch.astype(BF16), w_out=w_out.astype(BF16),
        w_gu=ffn_w_gu.astype(BF16), w_down=ffn_w_down.astype(BF16),
    )


def kernel(x, c, ctx, c_ctx, ada_w, ada_b, w_in, mla_q_norm, mla_w_uq, mla_kv_norm, mla_w_ukv, sgu_ln_g, sgu_ln_b, sgu_w, sgu_b, diff_lam, diff_subln, w_branch, w_out, ln1_g, ln1_b, ffn_w_gu, ffn_w_down, ln2_g, ln2_b):
    assert x.shape == (BATCH, SEQ, D_MODEL) and ctx.shape == (BATCH, CTX_LEN, D_MODEL)
    wts = _prepare_weights(w_in, mla_w_uq, mla_w_ukv, sgu_w, sgu_b, w_branch, w_out, ffn_w_gu, ffn_w_down)
    rope_tabs = _rope_tables(MLA_ROPE, PREP_TM) + _rope_tables(DIFF_HEAD_DIM, PREP_TM)

    c_all = jnp.concatenate([c, c_ctx[None], jnp.zeros((MOD_ROWS - BATCH - 1, D_MODEL), F32)], axis=0)
    mods = _mod_table(c_all, ada_w, ada_b).reshape(DEPTH, MOD_ROWS, 1, 6 * D_MODEL)

    h = jnp.concatenate([x.reshape(N_LAT, D_MODEL), ctx.reshape(N_CTX, D_MODEL)], axis=0)
    xm = _modulate(h, mods, 0)
    vec = lambda a: a.reshape(1, -1)
    for l in range(DEPTH):
        ctx_out = l < DEPTH - 1
        n_rows = N_TOK if ctx_out else N_LAT
        lam_init = 0.8 - 0.6 * math.exp(-0.3 * l)
        z = _in_proj(xm, wts["w_in"][l])
        qm, km, vm, qd, kd = _prep(z, xm, wts["w_kr"][l], vec(mla_q_norm[l]), vec(mla_kv_norm[l]),
                                   wts["w_uq"][l], wts["w_ukv"][l], rope_tabs)
        ya = _mla_attn(qm, km, vm, ctx_out)
        yd = _diff_attn(qd, kd, z, diff_lam[l], vec(diff_subln[l]), ctx_out, lam_init)
        ys = _sgu(z, vec(sgu_ln_g[l]), vec(sgu_ln_b[l]), wts["sgu_w"][l], wts["sgu_b_t"][l], n_rows)
        m = _branch_merge(ya, ys, yd, z, wts["w_branch"][l], n_rows)
        h1, hm = _out_proj(m, wts["w_out"][l], h, mods, l, vec(ln1_g[l]), vec(ln1_b[l]), n_rows)
        act = _ffn_up(hm, wts["w_gu"][l], n_rows)
        outs = _ffn_down(act, wts["w_down"][l], h1, mods, l, vec(ln2_g[l]), vec(ln2_b[l]), n_rows, ctx_out)
        if ctx_out:
            h, xm = outs
        else:
            h = outs[0]
    return h.reshape(BATCH, SEQ, D_MODEL)
```
